```python
import math
import jax, jax.numpy as jnp
from jax import lax
import numpy as np

D_MODEL = 1024
BATCH = 16
SEQ = 2048
DEPTH = 4

CHUNK = 64
A_LEFT_CHUNKS = 8
A_BAND = (A_LEFT_CHUNKS + 1) * CHUNK
HEAD_DIM = 64
MIX_WIDTH = D_MODEL
A_HEADS = 8
A_WIDTH = A_HEADS * HEAD_DIM
B_HEADS = 4
B_WIDTH = B_HEADS * 2 * HEAD_DIM
A_MAX_REL = 128
A_REL_SIZE = 2 * A_MAX_REL + 1
T5_BUCKETS = 32
T5_MAX_DIST = 128
D_FF = 2816
CONV_WIDTH = 3
Q_BLOCK = 128
RMS_EPS = 1e-6
IN_COLS = 3 * A_WIDTH + 3 * B_WIDTH

kernel_name = "hybrid_chunked_diff_convffn_trunk"


def rmsnorm(x, g):
    xf = x.astype(jnp.float32)
    y = xf * lax.rsqrt(jnp.mean(xf * xf, axis=-1, keepdims=True) + RMS_EPS)
    return (y * g.astype(jnp.float32)).astype(x.dtype)


def t5_bucket(rel):
    nb = T5_BUCKETS // 2
    ret = jnp.where(rel > 0, nb, 0)
    n = jnp.abs(rel)
    max_exact = nb // 2
    is_small = n < max_exact
    nf = jnp.maximum(n, 1).astype(jnp.float32)
    large = max_exact + (jnp.log(nf / max_exact) / math.log(T5_MAX_DIST / max_exact)
                         * (nb - max_exact)).astype(jnp.int32)
    large = jnp.minimum(large, nb - 1)
    return ret + jnp.where(is_small, n, large)


def chunked_band_attention(q, k, v, rel_table):
    b, s, h, dh = q.shape
    nc = s // CHUNK
    qc = q.reshape(b, nc, CHUNK, h, dh)
    pad = ((0, 0), (A_LEFT_CHUNKS * CHUNK, 0), (0, 0), (0, 0))
    kp = jnp.pad(k, pad).reshape(b, nc + A_LEFT_CHUNKS, CHUNK, h, dh)
    vp = jnp.pad(v, pad).reshape(b, nc + A_LEFT_CHUNKS, CHUNK, h, dh)
    kb = jnp.concatenate([kp[:, j:j + nc] for j in range(A_LEFT_CHUNKS + 1)], axis=2)
    vb = jnp.concatenate([vp[:, j:j + nc] for j in range(A_LEFT_CHUNKS + 1)], axis=2)
    scores = jnp.einsum("bcqhd,bckhd->bhcqk", qc, kb).astype(jnp.float32) * (dh ** -0.5)
    qq = jnp.arange(CHUNK)[:, None]
    m = jnp.arange(A_BAND)[None, :]
    dist = A_LEFT_CHUNKS * CHUNK + qq - m
    idx = jnp.clip(dist, -A_MAX_REL, A_MAX_REL) + A_MAX_REL
    bias = rel_table[:, idx].astype(jnp.float32)
    kpos = (jnp.arange(nc)[:, None] - A_LEFT_CHUNKS) * CHUNK + jnp.arange(A_BAND)[None, :]
    valid = (kpos >= 0)[None, None, :, None, :]
    scores = jnp.where(valid, scores + bias[:, None], -jnp.inf)
    p = jax.nn.softmax(scores, axis=-1).astype(v.dtype)
    o = jnp.einsum("bhcqk,bckhd->bcqhd", p, vb)
    return o.reshape(b, s, h * dh)


def diff_attention(q, k, v, t5_table, lam, subln_g, lam_init):
    b, s = q.shape[0], q.shape[1]
    nblk = s // Q_BLOCK
    qblocks = q.reshape(b, nblk, Q_BLOCK, 2 * B_HEADS, HEAD_DIM).transpose(1, 0, 2, 3, 4)
    kpos = jnp.arange(s)
    kchunk = kpos // CHUNK

    def block(args):
        qblk, i = args
        qpos = i * Q_BLOCK + jnp.arange(Q_BLOCK)
        sc = jnp.einsum("bqhd,bkhd->bhqk", qblk, k).astype(jnp.float32) * (HEAD_DIM ** -0.5)
        bucket = t5_bucket(kpos[None, :] - qpos[:, None])
        bias = jnp.transpose(t5_table[bucket], (2, 0, 1)).astype(jnp.float32)
        allowed = kchunk[None, :] <= (qpos // CHUNK)[:, None]
        sc = jnp.where(allowed, sc + bias, -jnp.inf)
        p = jax.nn.softmax(sc, axis=-1).reshape(b, B_HEADS, 2, Q_BLOCK, s)
        w = p[:, :, 0] - lam * p[:, :, 1]
        return jnp.einsum("bhqk,bkhe->bqhe", w.astype(v.dtype), v)

    o = lax.map(block, (qblocks, jnp.arange(nblk)))
    o = o.transpose(1, 0, 2, 3, 4).reshape(b, s, B_HEADS, 2 * HEAD_DIM)
    o = rmsnorm(o, subln_g) * (1.0 - lam_init)
    return o.reshape(b, s, B_WIDTH)


def causal_dwconv(u, w, bias):
    s = u.shape[1]
    up = jnp.pad(u, ((0, 0), (CONV_WIDTH - 1, 0), (0, 0)))
    out = bias + up[:, 0:s] * w[0]
    for j in range(1, CONV_WIDTH):
        out = out + up[:, j:j + s] * w[j]
    return out


def setup_inputs(seed: int = 0) -> dict:
    key = jax.random.key(seed)
    ks = jax.random.split(key, 20)
    f32 = jnp.float32
    nrm = lambda k, shape, scale: (jax.random.normal(k, shape, f32) * scale)
    return {
        "x": nrm(ks[0], (BATCH, SEQ, D_MODEL), 1.0),
        "attn_norm_g": 1.0 + nrm(ks[1], (DEPTH, D_MODEL), 0.02),
        "w_in": nrm(ks[2], (DEPTH, D_MODEL, IN_COLS), D_MODEL ** -0.5),
        "a_rel_bias": nrm(ks[3], (DEPTH, A_HEADS, A_REL_SIZE), 0.2),
        "t5_bias": nrm(ks[4], (T5_BUCKETS, 2 * B_HEADS), 0.2),
        "lambda_q1": nrm(ks[5], (DEPTH, HEAD_DIM), 0.1),
        "lambda_k1": nrm(ks[6], (DEPTH, HEAD_DIM), 0.1),
        "lambda_q2": nrm(ks[7], (DEPTH, HEAD_DIM), 0.1),
        "lambda_k2": nrm(ks[8], (DEPTH, HEAD_DIM), 0.1),
        "subln_g": 1.0 + nrm(ks[9], (DEPTH, 2 * HEAD_DIM), 0.02),
        "w_out": nrm(ks[10], (DEPTH, MIX_WIDTH, D_MODEL), MIX_WIDTH ** -0.5),
        "ffn_norm_g": 1.0 + nrm(ks[11], (DEPTH, D_MODEL), 0.02),
        "w_up": nrm(ks[12], (DEPTH, D_MODEL, 2 * D_FF), D_MODEL ** -0.5),
        "conv_w": nrm(ks[13], (DEPTH, CONV_WIDTH, 2 * D_FF), CONV_WIDTH ** -0.5),
        "conv_b": nrm(ks[14], (DEPTH, 2 * D_FF), 0.02),
        "w_down": nrm(ks[15], (DEPTH, D_FF, D_MODEL), D_FF ** -0.5),
        "final_norm_g": 1.0 + nrm(ks[16], (D_MODEL,), 0.02),
    }


def reference(x, attn_norm_g, w_in, a_rel_bias, t5_bias, lambda_q1, lambda_k1, lambda_q2,
              lambda_k2, subln_g, w_out, ffn_norm_g, w_up, conv_w, conv_b, w_down, final_norm_g):
    b, s, _ = x.shape
    for l in range(DEPTH):
        h = rmsnorm(x, attn_norm_g[l])
        proj = h @ w_in[l]
        qa, ka, va, qb, kb, vb = jnp.split(proj, 6, axis=-1)
        oa = chunked_band_attention(qa.reshape(b, s, A_HEADS, HEAD_DIM),
                                    ka.reshape(b, s, A_HEADS, HEAD_DIM),
                                    va.reshape(b, s, A_HEADS, HEAD_DIM),
                                    a_rel_bias[l])
        lam_init = 0.8 - 0.6 * math.exp(-0.3 * l)
        lam = (jnp.exp(jnp.sum(lambda_q1[l].astype(jnp.float32) * lambda_k1[l].astype(jnp.float32)))
               - jnp.exp(jnp.sum(lambda_q2[l].astype(jnp.float32) * lambda_k2[l].astype(jnp.float32)))
               + lam_init)
        ob = diff_attention(qb.reshape(b, s, 2 * B_HEADS, HEAD_DIM),
                            kb.reshape(b, s, 2 * B_HEADS, HEAD_DIM),
                            vb.reshape(b, s, B_HEADS, 2 * HEAD_DIM),
                            t5_bias, lam, subln_g[l], lam_init)
        x = x + jnp.concatenate([oa, ob], axis=-1) @ w_out[l]
        h = rmsnorm(x, ffn_norm_g[l])
        u = causal_dwconv(h @ w_up[l], conv_w[l], conv_b[l])
        gate, val = jnp.split(u, 2, axis=-1)
        x = x + (jax.nn.silu(gate) * val) @ w_down[l]
    return rmsnorm(x, final_norm_g)
```

```python
import functools
import math

import numpy as np
import jax
import jax.numpy as jnp
from jax import lax
from jax.experimental import pallas as pl
from jax.experimental.pallas import tpu as pltpu

CHUNK = 64
A_LEFT_CHUNKS = 8
HEAD_DIM = 64
A_HEADS = 8
B_HEADS = 4
A_MAX_REL = 128
T5_BUCKETS = 32
T5_MAX_DIST = 128
CONV_WIDTH = 3
RMS_EPS = 1e-6

LANES = 128
SUBLANES = 8
PAIR = 2 * HEAD_DIM
MASK_VALUE = -1e30

QA = 128
WA = A_LEFT_CHUNKS * CHUNK + QA
PAD_A = A_LEFT_CHUNKS * CHUNK
QB = 256
TM_PROJ = 512
TM_FFN = 512
TF = 256
EXPAND_COLS = 2048

VMEM_LIMIT = 56 * 1024 * 1024


def _dot(a, b):
    return jnp.dot(a, b, preferred_element_type=jnp.float32)


def _dot_nt(a, b):
    return lax.dot_general(a, b, (((1,), (1,)), ((), ())),
                           preferred_element_type=jnp.float32)


def _rmsnorm(x, g):
    return x * lax.rsqrt(jnp.mean(x * x, axis=-1, keepdims=True) + RMS_EPS) * g


def _expand_kernel(tab_ref, idx_ref, o_ref, *, rows):
    n_entries = tab_ref.shape[1]
    cols = idx_ref.shape[1]
    entry = lax.broadcasted_iota(jnp.int32, (n_entries, cols), 0)
    onehot = jnp.where(entry == idx_ref[...], 1.0, 0.0).astype(jnp.bfloat16)
    parts = _dot(tab_ref[...], onehot)
    o_ref[...] = (parts[0:rows] + parts[rows:2 * rows]) + parts[2 * rows:3 * rows]


def _expand(table, idx, name):
    rows, n_entries = table.shape
    ncols = idx.shape[1]
    hi = table.astype(jnp.bfloat16)
    r1 = table - hi.astype(jnp.float32)
    mid = r1.astype(jnp.bfloat16)
    lo = (r1 - mid.astype(jnp.float32)).astype(jnp.bfloat16)
    tab3 = jnp.concatenate([hi, mid, lo], axis=0)
    return pl.pallas_call(
        functools.partial(_expand_kernel, rows=rows),
        out_shape=jax.ShapeDtypeStruct((rows, ncols), jnp.float32),
        grid=(ncols // EXPAND_COLS,),
        in_specs=[pl.BlockSpec((3 * rows, n_entries), lambda c: (0, 0)),
                  pl.BlockSpec((1, EXPAND_COLS), lambda c: (0, c))],
        out_specs=pl.BlockSpec((rows, EXPAND_COLS), lambda c: (0, c)),
        compiler_params=pltpu.CompilerParams(dimension_semantics=("arbitrary",)),
        name=name,
    )(tab3, idx)


def _pad_entries(table, mask_col):
    rows, n = table.shape
    total = -(-(n + 1) // LANES) * LANES
    out = jnp.zeros((rows, total), jnp.float32)
    out = out.at[:, :n].set(table)
    return out.at[:, mask_col].set(MASK_VALUE)


def _band_bias_index():
    q = np.arange(QA)[:, None]
    m = np.arange(WA)[None, :]
    dist = PAD_A + q - m
    idx = np.clip(dist, -A_MAX_REL, A_MAX_REL) + A_MAX_REL
    qc, kc = q // CHUNK, m // CHUNK
    visible = (kc >= qc) & (kc <= qc + A_LEFT_CHUNKS)
    mask_entry = 2 * A_MAX_REL + 1
    return np.where(visible, idx, mask_entry).astype(np.int32).reshape(1, QA * WA)


def _t5_bucket(rel):
    nb = T5_BUCKETS // 2
    ret = jnp.where(rel > 0, nb, 0)
    n = jnp.abs(rel)
    max_exact = nb // 2
    is_small = n < max_exact
    nf = jnp.maximum(n, 1).astype(jnp.float32)
    large = max_exact + (jnp.log(nf / max_exact) / math.log(T5_MAX_DIST / max_exact)
                         * (nb - max_exact)).astype(jnp.int32)
    large = jnp.minimum(large, nb - 1)
    return ret + jnp.where(is_small, n, large)


def _t5_bias_index():
    q = np.arange(QB)[:, None]
    k = np.arange(QB)[None, :]
    rel_d = (k - q).astype(np.int32)
    rel_p = (k - QB - q).astype(np.int32)
    diag = jnp.where(jnp.asarray(k // CHUNK <= q // CHUNK), _t5_bucket(jnp.asarray(rel_d)),
                     T5_BUCKETS)
    prev = _t5_bucket(jnp.asarray(rel_p))
    far = jnp.broadcast_to(_t5_bucket(jnp.asarray([-(QB + 1)], dtype=jnp.int32)),
                           (EXPAND_COLS,))
    return jnp.concatenate([diag.reshape(-1), prev.reshape(-1), far]).astype(jnp.int32)[None]


def _inproj_kernel(x_ref, g_ref, w_ref, o_ref, *, col_chunk):
    h = _rmsnorm(x_ref[...], g_ref[...]).astype(jnp.bfloat16)
    for c in range(w_ref.shape[1] // col_chunk):
        sl = slice(c * col_chunk, (c + 1) * col_chunk)
        o_ref[:, sl] = _dot(h, w_ref[:, sl]).astype(o_ref.dtype)


def _inproj(x2d, g, w, layer):
    t, d = x2d.shape
    n = w.shape[2]
    return pl.pallas_call(
        functools.partial(_inproj_kernel, col_chunk=512),
        out_shape=jax.ShapeDtypeStruct((t, n), jnp.bfloat16),
        grid=(t // TM_PROJ,),
        in_specs=[pl.BlockSpec((TM_PROJ, d), lambda i: (i, 0)),
                  pl.BlockSpec((None, 1, d), lambda i: (layer, 0, 0)),
                  pl.BlockSpec((None, d, n), lambda i: (layer, 0, 0))],
        out_specs=pl.BlockSpec((TM_PROJ, n), lambda i: (i, 0)),
        compiler_params=pltpu.CompilerParams(dimension_semantics=("arbitrary",),
                                             vmem_limit_bytes=VMEM_LIMIT),
        name="inproj",
    )(x2d, g, w)


def _split_heads(q2):
    lane = lax.broadcasted_iota(jnp.int32, q2.shape, 1)
    zero = jnp.zeros_like(q2)
    return jnp.concatenate([jnp.where(lane < HEAD_DIM, q2, zero),
                            jnp.where(lane >= HEAD_DIM, q2, zero)], axis=0)


def _band_kernel(q_ref, k_ref, v_ref, bias_ref, o_ref, kpad, vpad):
    seq = q_ref.shape[1]
    zeros = jnp.zeros((PAD_A, PAIR), kpad.dtype)
    kpad[0:PAD_A, :] = zeros
    vpad[0:PAD_A, :] = zeros
    kpad[PAD_A:PAD_A + seq, :] = k_ref[0]
    vpad[PAD_A:PAD_A + seq, :] = v_ref[0]
    scale = HEAD_DIM ** -0.5

    def block(i, carry, *, has_padding):
        r0 = pl.multiple_of(i * QA, QA)
        q2 = q_ref[0, pl.ds(r0, QA), :] * scale
        qs = _split_heads(q2)
        s = _dot_nt(qs, kpad[pl.ds(r0, WA), :]) + bias_ref[...]
        if has_padding:
            col = lax.broadcasted_iota(jnp.int32, s.shape, 1)
            s = jnp.where(col >= PAD_A - r0, s, MASK_VALUE)
        m = jnp.max(s, axis=1, keepdims=True)
        e = jnp.exp(s - m)
        l = jnp.sum(e, axis=1, keepdims=True)
        pv = _dot(e.astype(jnp.bfloat16), vpad[pl.ds(r0, WA), :]) / l
        lane = lax.broadcasted_iota(jnp.int32, (QA, PAIR), 1)
        o = jnp.where(lane < HEAD_DIM, pv[0:QA], pv[QA:2 * QA])
        o_ref[0, pl.ds(r0, QA), :] = o.astype(o_ref.dtype)
        return carry

    n_blocks = seq // QA
    n_padded = min(PAD_A // QA, n_blocks)
    lax.fori_loop(0, n_padded, functools.partial(block, has_padding=True), 0)
    lax.fori_loop(n_padded, n_blocks, functools.partial(block, has_padding=False), 0)


def _band_attention(proj, bias, layer):
    b, s, _ = proj.shape
    pairs = A_HEADS // 2
    slab = lambda part: pl.BlockSpec((1, s, PAIR), lambda bi, h: (bi, 0, part * pairs + h))
    return pl.pallas_call(
        _band_kernel,
        out_shape=jax.ShapeDtypeStruct((b, s, A_HEADS * HEAD_DIM), jnp.bfloat16),
        grid=(b, pairs),
        in_specs=[slab(0), slab(1), slab(2),
                  pl.BlockSpec((None, None, 2 * QA, WA), lambda bi, h: (layer, h, 0, 0))],
        out_specs=pl.BlockSpec((1, s, PAIR), lambda bi, h: (bi, 0, h)),
        scratch_shapes=[pltpu.VMEM((PAD_A + s, PAIR), jnp.bfloat16),
                        pltpu.VMEM((PAD_A + s, PAIR), jnp.bfloat16)],
        compiler_params=pltpu.CompilerParams(dimension_semantics=("arbitrary", "arbitrary"),
                                             vmem_limit_bytes=VMEM_LIMIT),
        name="band_attn",
    )(proj, proj, proj, bias)


def _diff_kernel(q_ref, k_ref, v_ref, bd_ref, bp_ref, cf_ref, lam_ref, g_ref, o_ref,
                 m_ref, l_ref, acc_ref, *, lam_init):
    seq = q_ref.shape[1]
    scale = HEAD_DIM ** -0.5
    lp = lam_ref[...]
    lam = (jnp.exp(jnp.sum(lp[0:1] * lp[1:2], axis=1, keepdims=True))
           - jnp.exp(jnp.sum(lp[2:3] * lp[3:4], axis=1, keepdims=True)) + lam_init)
    far_bias = cf_ref[:, 0:1]

    def scores(qs, start):
        return _dot_nt(qs, k_ref[0, pl.ds(start, QB), :])

    def accumulate(s, start):
        m_old = m_ref[...]
        m_new = jnp.maximum(m_old, jnp.max(s, axis=1, keepdims=True))
        alpha = jnp.exp(m_old - m_new)
        e = jnp.exp(s - m_new)
        l_ref[...] = alpha * l_ref[...] + jnp.sum(e, axis=1, keepdims=True)
        acc_ref[...] = alpha * acc_ref[...] + _dot(e.astype(jnp.bfloat16),
                                                    v_ref[0, pl.ds(start, QB), :])
        m_ref[...] = m_new

    def block(i, carry):
        r0 = pl.multiple_of(i * QB, QB)
        qs = _split_heads(q_ref[0, pl.ds(r0, QB), :] * scale)

        s = scores(qs, r0) + bd_ref[...]
        m = jnp.max(s, axis=1, keepdims=True)
        e = jnp.exp(s - m)
        m_ref[...] = m
        l_ref[...] = jnp.sum(e, axis=1, keepdims=True)
        acc_ref[...] = _dot(e.astype(jnp.bfloat16), v_ref[0, pl.ds(r0, QB), :])

        @pl.when(i > 0)
        def _():
            p0 = pl.multiple_of(r0 - QB, QB)
            accumulate(scores(qs, p0) + bp_ref[...], p0)

        def far(t, c):
            f0 = pl.multiple_of(t * QB, QB)
            accumulate(scores(qs, f0) + far_bias, f0)
            return c

        lax.fori_loop(0, i - 1, far, 0)

        o = acc_ref[...] / l_ref[...]
        o = o[0:QB] - lam * o[QB:2 * QB]
        o = _rmsnorm(o, g_ref[...]) * (1.0 - lam_init)
        o_ref[0, pl.ds(r0, QB), :] = o.astype(o_ref.dtype)
        return carry

    lax.fori_loop(0, seq // QB, block, 0)


def _diff_attention(proj, bias_d, bias_p, bias_far, lam_params, subln_g, layer, lam_init):
    b, s, _ = proj.shape
    first = 3 * A_HEADS * HEAD_DIM // PAIR
    slab = lambda part: pl.BlockSpec((1, s, PAIR),
                                     lambda bi, h: (bi, 0, first + part * B_HEADS + h))
    tile = lambda cols: pl.BlockSpec((None, 2 * QB, cols), lambda bi, h: (h, 0, 0))
    return pl.pallas_call(
        functools.partial(_diff_kernel, lam_init=lam_init),
        out_shape=jax.ShapeDtypeStruct((b, s, B_HEADS * PAIR), jnp.bfloat16),
        grid=(b, B_HEADS),
        in_specs=[slab(0), slab(1), slab(2), tile(QB), tile(QB), tile(LANES),
                  pl.BlockSpec((None, 4, HEAD_DIM), lambda bi, h: (layer, 0, 0)),
                  pl.BlockSpec((None, 1, PAIR), lambda bi, h: (layer, 0, 0))],
        out_specs=pl.BlockSpec((1, s, PAIR), lambda bi, h: (bi, 0, h)),
        scratch_shapes=[pltpu.VMEM((2 * QB, 1), jnp.float32),
                        pltpu.VMEM((2 * QB, 1), jnp.float32),
                        pltpu.VMEM((2 * QB, PAIR), jnp.float32)],
        compiler_params=pltpu.CompilerParams(dimension_semantics=("arbitrary", "arbitrary"),
                                             vmem_limit_bytes=VMEM_LIMIT),
        name="diff_attn",
    )(proj, proj, proj, bias_d, bias_p, bias_far, lam_params, subln_g)


def _ffn_kernel(*refs, tiles_per_seq, final_norm):
    if final_norm:
        (x_ref, oa_ref, ob_ref, wout_ref, g_ref, wup_ref, cw_ref, cb_ref, wd_ref, fg_ref,
         out_ref, h_s, acc_s, ug_s, uv_s, carry_s) = refs
    else:
        (x_ref, oa_ref, ob_ref, wout_ref, g_ref, wup_ref, cw_ref, cb_ref, wd_ref,
         out_ref, h_s, acc_s, ug_s, uv_s, carry_s) = refs
    tm = x_ref.shape[0]
    n_blocks = wd_ref.shape[0]
    wa = oa_ref.shape[1]
    halo = SUBLANES

    @pl.when(pl.program_id(0) % tiles_per_seq == 0)
    def _():
        carry_s[...] = jnp.zeros_like(carry_s)

    x1 = x_ref[...] + _dot(oa_ref[...], wout_ref[0:wa, :]) + _dot(ob_ref[...], wout_ref[wa:, :])
    h_s[...] = _rmsnorm(x1, g_ref[...]).astype(h_s.dtype)
    acc_s[...] = x1

    def conv(up, up_s, part, j):
        up_s[0:halo, :] = carry_s[part, j]
        up_s[halo:halo + tm, :] = up
        carry_s[part, j] = up[tm - halo:tm, :]
        tap = lambda t: cw_ref[t, part, pl.ds(j, 1), :]
        out = cb_ref[part, pl.ds(j, 1), :] + up_s[halo - 2:halo - 2 + tm, :] * tap(0)
        out = out + up_s[halo - 1:halo - 1 + tm, :] * tap(1)
        return out + up * tap(2)

    def block(j, carry):
        h = h_s[...]
        gate = conv(_dot(h, wup_ref[0, j]), ug_s, 0, j)
        val = conv(_dot(h, wup_ref[1, j]), uv_s, 1, j)
        act = (gate * (1.0 / (1.0 + jnp.exp(-gate))) * val).astype(jnp.bfloat16)
        acc_s[...] += _dot(act, wd_ref[j])
        return carry

    lax.fori_loop(0, n_blocks, block, 0)
    out = acc_s[...]
    if final_norm:
        out = _rmsnorm(out, fg_ref[...])
    out_ref[...] = out


def _ffn(x2d, oa, ob, wout, g, wup, cw, cb, wd, final_g, layer, seq):
    t, d = x2d.shape
    wa = oa.shape[1]
    n_blocks = wd.shape[1]
    resident = dict(pipeline_mode=pl.Buffered(1))
    row = lambda cols: pl.BlockSpec((TM_FFN, cols), lambda i: (i, 0))
    in_specs = [row(d), row(wa), row(ob.shape[1]),
                pl.BlockSpec((None, d, d), lambda i: (layer, 0, 0), **resident),
                pl.BlockSpec((None, 1, d), lambda i: (layer, 0, 0)),
                pl.BlockSpec((None, 2, n_blocks, d, TF), lambda i: (layer, 0, 0, 0, 0), **resident),
                pl.BlockSpec((None, CONV_WIDTH, 2, n_blocks, TF), lambda i: (layer, 0, 0, 0, 0)),
                pl.BlockSpec((None, 2, n_blocks, TF), lambda i: (layer, 0, 0, 0)),
                pl.BlockSpec((None, n_blocks, TF, d), lambda i: (layer, 0, 0, 0), **resident)]
    args = [x2d, oa, ob, wout, g, wup, cw, cb, wd]
    if final_g is not None:
        in_specs.append(pl.BlockSpec((1, d), lambda i: (0, 0)))
        args.append(final_g)
    return pl.pallas_call(
        functools.partial(_ffn_kernel, tiles_per_seq=seq // TM_FFN,
                          final_norm=final_g is not None),
        out_shape=jax.ShapeDtypeStruct((t, d), jnp.float32),
        grid=(t // TM_FFN,),
        in_specs=in_specs,
        out_specs=row(d),
        scratch_shapes=[pltpu.VMEM((TM_FFN, d), jnp.bfloat16),
                        pltpu.VMEM((TM_FFN, d), jnp.float32),
                        pltpu.VMEM((SUBLANES + TM_FFN, TF), jnp.float32),
                        pltpu.VMEM((SUBLANES + TM_FFN, TF), jnp.float32),
                        pltpu.VMEM((2, n_blocks, SUBLANES, TF), jnp.float32)],
        compiler_params=pltpu.CompilerParams(dimension_semantics=("arbitrary",),
                                             vmem_limit_bytes=VMEM_LIMIT),
        name="outproj_convffn",
    )(*args)


def kernel(x, attn_norm_g, w_in, a_rel_bias, t5_bias, lambda_q1, lambda_k1, lambda_q2,
           lambda_k2, subln_g, w_out, ffn_norm_g, w_up, conv_w, conv_b, w_down, final_norm_g):
    b, s, d = x.shape
    depth = w_in.shape[0]
    d_ff = w_down.shape[1]
    n_blocks = d_ff // TF
    assert s % QB == 0 and s % TM_FFN == 0 and d_ff % TF == 0
    bf16 = jnp.bfloat16

    w_in_b = w_in.astype(bf16)
    w_out_b = w_out.astype(bf16)
    w_up_b = w_up.astype(bf16).reshape(depth, d, 2, n_blocks, TF).transpose(0, 2, 3, 1, 4)
    w_down_b = w_down.astype(bf16).reshape(depth, n_blocks, TF, d)
    conv_w_r = conv_w.reshape(depth, CONV_WIDTH, 2, n_blocks, TF)
    conv_b_r = conv_b.reshape(depth, 2, n_blocks, TF)
    attn_g = attn_norm_g.reshape(depth, 1, d)
    ffn_g = ffn_norm_g.reshape(depth, 1, d)
    sub_g = subln_g.reshape(depth, 1, PAIR)
    lam_params = jnp.stack([lambda_q1, lambda_k1, lambda_q2, lambda_k2], axis=1)

    band_tab = _pad_entries(a_rel_bias.reshape(depth * A_HEADS, -1), 2 * A_MAX_REL + 1)
    band_bias = _expand(band_tab, jnp.asarray(_band_bias_index()), "expand_band_bias")
    band_bias = band_bias.reshape(depth, A_HEADS // 2, 2 * QA, WA)
    t5_tab = _pad_entries(t5_bias.T, T5_BUCKETS)
    t5_exp = _expand(t5_tab, _t5_bias_index(), "expand_t5_bias")
    bias_d = t5_exp[:, :QB * QB].reshape(B_HEADS, 2 * QB, QB)
    bias_p = t5_exp[:, QB * QB:2 * QB * QB].reshape(B_HEADS, 2 * QB, QB)
    bias_far = jnp.broadcast_to(t5_exp[:, None, 2 * QB * QB:2 * QB * QB + LANES],
                                (2 * B_HEADS, QB, LANES)).reshape(B_HEADS, 2 * QB, LANES)

    x2d = x.reshape(b * s, d)
    for layer in range(depth):
        lam_init = 0.8 - 0.6 * math.exp(-0.3 * layer)
        proj = _inproj(x2d, attn_g, w_in_b, layer).reshape(b, s, -1)
        oa = _band_attention(proj, band_bias, layer).reshape(b * s, -1)
        ob = _diff_attention(proj, bias_d, bias_p, bias_far, lam_params, sub_g, layer,
                             lam_init).reshape(b * s, -1)
        final_g = final_norm_g.reshape(1, d) if layer == depth - 1 else None
        x2d = _ffn(x2d, oa, ob, w_out_b, ffn_g, w_up_b, conv_w_r, conv_b_r, w_down_b,
                   final_g, layer, s)
    return x2d.reshape(b, s, d)
```

```python
import functools
import math

import numpy as np
import jax
import jax.numpy as jnp
from jax import lax
from jax.experimental import pallas as pl
from jax.experimental.pallas import tpu as pltpu

CHUNK = 64
A_LEFT_CHUNKS = 8
HEAD_DIM = 64
A_HEADS = 8
B_HEADS = 4
A_MAX_REL = 128
T5_BUCKETS = 32
T5_MAX_DIST = 128
CONV_WIDTH = 3
RMS_EPS = 1e-6

LANES = 128
SUBLANES = 8
PAIR = 2 * HEAD_DIM
MASK_VALUE = -1e30

A_WIDTH = A_HEADS * HEAD_DIM
B_WIDTH = B_HEADS * PAIR
QA = 128
WA = A_LEFT_CHUNKS * CHUNK + QA
PAD_A = A_LEFT_CHUNKS * CHUNK
QB = 256
TM_PROJ = 512
TM_FFN = 512
TF = 256
EXPAND_COLS = 2048

VMEM_LIMIT = 56 * 1024 * 1024


def _dot(a, b):
    return jnp.dot(a, b, preferred_element_type=jnp.float32)


def _dot_nt(a, b):
    return lax.dot_general(a, b, (((1,), (1,)), ((), ())),
                           preferred_element_type=jnp.float32)


def _rmsnorm(x, g):
    return x * lax.rsqrt(jnp.mean(x * x, axis=-1, keepdims=True) + RMS_EPS) * g


def _expand_kernel(tab_ref, idx_ref, o_ref, *, rows):
    n_entries = tab_ref.shape[1]
    cols = idx_ref.shape[1]
    entry = lax.broadcasted_iota(jnp.int32, (n_entries, cols), 0)
    onehot = jnp.where(entry == idx_ref[...], 1.0, 0.0).astype(jnp.bfloat16)
    parts = _dot(tab_ref[...], onehot)
    o_ref[...] = (parts[0:rows] + parts[rows:2 * rows]) + parts[2 * rows:3 * rows]


def _expand(table, idx, name):
    rows, n_entries = table.shape
    ncols = idx.shape[1]
    hi = table.astype(jnp.bfloat16)
    r1 = table - hi.astype(jnp.float32)
    mid = r1.astype(jnp.bfloat16)
    lo = (r1 - mid.astype(jnp.float32)).astype(jnp.bfloat16)
    tab3 = jnp.concatenate([hi, mid, lo], axis=0)
    return pl.pallas_call(
        functools.partial(_expand_kernel, rows=rows),
        out_shape=jax.ShapeDtypeStruct((rows, ncols), jnp.float32),
        grid=(ncols // EXPAND_COLS,),
        in_specs=[pl.BlockSpec((3 * rows, n_entries), lambda c: (0, 0)),
                  pl.BlockSpec((1, EXPAND_COLS), lambda c: (0, c))],
        out_specs=pl.BlockSpec((rows, EXPAND_COLS), lambda c: (0, c)),
        compiler_params=pltpu.CompilerParams(dimension_semantics=("arbitrary",)),
        name=name,
    )(tab3, idx)


def _pad_entries(table, mask_col):
    rows, n = table.shape
    total = -(-(n + 1) // LANES) * LANES
    out = jnp.zeros((rows, total), jnp.float32)
    out = out.at[:, :n].set(table)
    return out.at[:, mask_col].set(MASK_VALUE)


def _band_bias_index():
    m = np.arange(WA)[:, None]
    q = np.arange(QA)[None, :]
    dist = PAD_A + q - m
    idx = np.clip(dist, -A_MAX_REL, A_MAX_REL) + A_MAX_REL
    qc, kc = q // CHUNK, m // CHUNK
    visible = (kc >= qc) & (kc <= qc + A_LEFT_CHUNKS)
    mask_entry = 2 * A_MAX_REL + 1
    return np.where(visible, idx, mask_entry).astype(np.int32).reshape(1, WA * QA)


def _t5_bucket(rel):
    nb = T5_BUCKETS // 2
    ret = jnp.where(rel > 0, nb, 0)
    n = jnp.abs(rel)
    max_exact = nb // 2
    is_small = n < max_exact
    nf = jnp.maximum(n, 1).astype(jnp.float32)
    large = max_exact + (jnp.log(nf / max_exact) / math.log(T5_MAX_DIST / max_exact)
                         * (nb - max_exact)).astype(jnp.int32)
    large = jnp.minimum(large, nb - 1)
    return ret + jnp.where(is_small, n, large)


def _t5_bias_index():
    k = np.arange(QB)[:, None]
    q = np.arange(QB)[None, :]
    rel_d = (k - q).astype(np.int32)
    rel_p = (k - QB - q).astype(np.int32)
    diag = jnp.where(jnp.asarray(k // CHUNK <= q // CHUNK), _t5_bucket(jnp.asarray(rel_d)),
                     T5_BUCKETS)
    prev = _t5_bucket(jnp.asarray(rel_p))
    far = jnp.broadcast_to(_t5_bucket(jnp.asarray([-(QB + 1)], dtype=jnp.int32)),
                           (EXPAND_COLS,))
    return jnp.concatenate([diag.reshape(-1), prev.reshape(-1), far]).astype(jnp.int32)[None]


def _inproj_kernel(x_ref, g_ref, wqk_ref, wvt_ref, qk_ref, vt_ref, *, chunk):
    h = _rmsnorm(x_ref[...], g_ref[...]).astype(jnp.bfloat16)
    for c in range(wqk_ref.shape[1] // chunk):
        sl = slice(c * chunk, (c + 1) * chunk)
        qk_ref[:, sl] = _dot(h, wqk_ref[:, sl]).astype(qk_ref.dtype)
    for c in range(wvt_ref.shape[0] // chunk):
        sl = slice(c * chunk, (c + 1) * chunk)
        vt_ref[0, sl, :] = _dot_nt(wvt_ref[sl, :], h).astype(vt_ref.dtype)


def _inproj(x2d, g, wqk, wvt, layer, batch, seq):
    t, d = x2d.shape
    nqk, nv = wqk.shape[2], wvt.shape[1]
    tiles = seq // TM_PROJ
    return pl.pallas_call(
        functools.partial(_inproj_kernel, chunk=512),
        out_shape=(jax.ShapeDtypeStruct((t, nqk), jnp.bfloat16),
                   jax.ShapeDtypeStruct((batch, nv, seq), jnp.bfloat16)),
        grid=(t // TM_PROJ,),
        in_specs=[pl.BlockSpec((TM_PROJ, d), lambda i: (i, 0)),
                  pl.BlockSpec((None, 1, d), lambda i: (layer, 0, 0)),
                  pl.BlockSpec((None, d, nqk), lambda i: (layer, 0, 0)),
                  pl.BlockSpec((None, nv, d), lambda i: (layer, 0, 0))],
        out_specs=(pl.BlockSpec((TM_PROJ, nqk), lambda i: (i, 0)),
                   pl.BlockSpec((1, nv, TM_PROJ), lambda i: (i // tiles, 0, i % tiles))),
        compiler_params=pltpu.CompilerParams(dimension_semantics=("arbitrary",),
                                             vmem_limit_bytes=VMEM_LIMIT),
        name="inproj",
    )(x2d, g, wqk, wvt)


def _split_heads(q2):
    lane = lax.broadcasted_iota(jnp.int32, q2.shape, 1)
    zero = jnp.zeros_like(q2)
    return jnp.concatenate([jnp.where(lane < HEAD_DIM, q2, zero),
                            jnp.where(lane >= HEAD_DIM, q2, zero)], axis=0)


def _band_kernel(q_ref, k_ref, vt_ref, bias_ref, o_ref):
    seq = q_ref.shape[1]
    scale = HEAD_DIM ** -0.5
    head_row = lax.broadcasted_iota(jnp.int32, (PAIR, QA), 0) < HEAD_DIM
    for i in range(seq // QA):
        lo, hi = max(0, i * QA - PAD_A), (i + 1) * QA
        n = hi - lo
        qs = _split_heads(q_ref[0, i * QA:hi, :] * scale)
        s = _dot_nt(k_ref[0, lo:hi, :], qs) + bias_ref[WA - n:WA, :]
        m = jnp.max(s, axis=0, keepdims=True)
        e = jnp.exp(s - m)
        l = jnp.sum(e, axis=0, keepdims=True)
        acc = _dot(vt_ref[0, :, lo:hi], e.astype(jnp.bfloat16))
        ot = acc * (1.0 / l)
        o = jnp.where(head_row, ot[:, 0:QA], ot[:, QA:2 * QA]).T
        o_ref[0, i * QA:hi, :] = o.astype(o_ref.dtype)


def _band_attention(qk, vt, bias, layer):
    b, s, _ = qk.shape
    pairs = A_HEADS // 2
    return pl.pallas_call(
        _band_kernel,
        out_shape=jax.ShapeDtypeStruct((b, s, A_WIDTH), jnp.bfloat16),
        grid=(b, pairs),
        in_specs=[pl.BlockSpec((1, s, PAIR), lambda bi, h: (bi, 0, h)),
                  pl.BlockSpec((1, s, PAIR), lambda bi, h: (bi, 0, pairs + h)),
                  pl.BlockSpec((1, PAIR, s), lambda bi, h: (bi, h, 0)),
                  pl.BlockSpec((None, None, WA, 2 * QA), lambda bi, h: (layer, h, 0, 0))],
        out_specs=pl.BlockSpec((1, s, PAIR), lambda bi, h: (bi, 0, h)),
        compiler_params=pltpu.CompilerParams(dimension_semantics=("arbitrary", "arbitrary"),
                                             vmem_limit_bytes=VMEM_LIMIT),
        name="band_attn",
    )(qk, qk, vt, bias)


def _diff_kernel(q_ref, k_ref, vt_ref, bd_ref, bp_ref, cf_ref, lam_ref, g_ref, o_ref,
                 *, lam_init):
    seq = q_ref.shape[1]
    scale = HEAD_DIM ** -0.5
    lp = lam_ref[...]
    lam = (jnp.exp(jnp.sum(lp[0:1] * lp[1:2], axis=1, keepdims=True))
           - jnp.exp(jnp.sum(lp[2:3] * lp[3:4], axis=1, keepdims=True)) + lam_init)
    far_bias = cf_ref[...]
    for i in range(seq // QB):
        klen = (i + 1) * QB
        near0 = max(0, klen - 2 * QB)
        qs = _split_heads(q_ref[0, i * QB:klen, :] * scale)
        s = _dot_nt(k_ref[0, 0:klen, :], qs)
        if i == 0:
            s_near = s + bd_ref[...]
        else:
            s_near = s[near0:klen] + jnp.concatenate([bp_ref[...], bd_ref[...]], axis=0)
        m = jnp.max(s_near, axis=0, keepdims=True)
        if near0 > 0:
            s_far = s[0:near0]
            m = jnp.maximum(m, jnp.max(s_far, axis=0, keepdims=True) + far_bias)
        e_near = jnp.exp(s_near - m)
        l = jnp.sum(e_near, axis=0, keepdims=True)
        acc = _dot(vt_ref[0, :, near0:klen], e_near.astype(jnp.bfloat16))
        if near0 > 0:
            e_far = jnp.exp(s_far - (m - far_bias))
            l = l + jnp.sum(e_far, axis=0, keepdims=True)
            acc = acc + _dot(vt_ref[0, :, 0:near0], e_far.astype(jnp.bfloat16))
        ot = acc * (1.0 / l)
        o = (ot[:, 0:QB] - lam * ot[:, QB:2 * QB]).T
        o = _rmsnorm(o, g_ref[...]) * (1.0 - lam_init)
        o_ref[0, i * QB:klen, :] = o.astype(o_ref.dtype)


def _diff_attention(qk, vt, bias_d, bias_p, bias_far, lam_params, subln_g, layer, lam_init):
    b, s, _ = qk.shape
    first = 2 * A_WIDTH // PAIR
    tile = pl.BlockSpec((None, QB, 2 * QB), lambda bi, h: (h, 0, 0))
    return pl.pallas_call(
        functools.partial(_diff_kernel, lam_init=lam_init),
        out_shape=jax.ShapeDtypeStruct((b, s, B_WIDTH), jnp.bfloat16),
        grid=(b, B_HEADS),
        in_specs=[pl.BlockSpec((1, s, PAIR), lambda bi, h: (bi, 0, first + h)),
                  pl.BlockSpec((1, s, PAIR), lambda bi, h: (bi, 0, first + B_HEADS + h)),
                  pl.BlockSpec((1, PAIR, s), lambda bi, h: (bi, A_WIDTH // PAIR + h, 0)),
                  tile, tile,
                  pl.BlockSpec((None, 1, 2 * QB), lambda bi, h: (h, 0, 0)),
                  pl.BlockSpec((None, 4, HEAD_DIM), lambda bi, h: (layer, 0, 0)),
                  pl.BlockSpec((None, 1, PAIR), lambda bi, h: (layer, 0, 0))],
        out_specs=pl.BlockSpec((1, s, PAIR), lambda bi, h: (bi, 0, h)),
        compiler_params=pltpu.CompilerParams(dimension_semantics=("arbitrary", "arbitrary"),
                                             vmem_limit_bytes=VMEM_LIMIT),
        name="diff_attn",
    )(qk, qk, vt, bias_d, bias_p, bias_far, lam_params, subln_g)


def _ffn_kernel(*refs, tiles_per_seq, final_norm):
    if final_norm:
        (x_ref, oa_ref, ob_ref, wout_ref, g_ref, wup_ref, cw_ref, cb_ref, wd_ref, fg_ref,
         out_ref, h_s, acc_s, ug_s, uv_s, carry_s) = refs
    else:
        (x_ref, oa_ref, ob_ref, wout_ref, g_ref, wup_ref, cw_ref, cb_ref, wd_ref,
         out_ref, h_s, acc_s, ug_s, uv_s, carry_s) = refs
    tm = x_ref.shape[0]
    n_blocks = wd_ref.shape[0]
    wa = oa_ref.shape[1]
    halo = SUBLANES

    @pl.when(pl.program_id(0) % tiles_per_seq == 0)
    def _():
        carry_s[...] = jnp.zeros_like(carry_s)

    x1 = x_ref[...] + _dot(oa_ref[...], wout_ref[0:wa, :]) + _dot(ob_ref[...], wout_ref[wa:, :])
    h_s[...] = _rmsnorm(x1, g_ref[...]).astype(h_s.dtype)
    acc_s[...] = x1

    def conv(up, up_s, part, j):
        up_s[0:halo, :] = carry_s[part, j]
        up_s[halo:halo + tm, :] = up
        carry_s[part, j] = up[tm - halo:tm, :]
        tap = lambda t: cw_ref[t, part, pl.ds(j, 1), :]
        out = cb_ref[part, pl.ds(j, 1), :] + up_s[halo - 2:halo - 2 + tm, :] * tap(0)
        out = out + up_s[halo - 1:halo - 1 + tm, :] * tap(1)
        return out + up * tap(2)

    def block(j, carry):
        h = h_s[...]
        gate = conv(_dot(h, wup_ref[0, j]), ug_s, 0, j)
        val = conv(_dot(h, wup_ref[1, j]), uv_s, 1, j)
        act = (gate * (1.0 / (1.0 + jnp.exp(-gate))) * val).astype(jnp.bfloat16)
        acc_s[...] += _dot(act, wd_ref[j])
        return carry

    lax.fori_loop(0, n_blocks, block, 0)
    out = acc_s[...]
    if final_norm:
        out = _rmsnorm(out, fg_ref[...])
    out_ref[...] = out


def _ffn(x2d, oa, ob, wout, g, wup, cw, cb, wd, final_g, layer, seq):
    t, d = x2d.shape
    wa = oa.shape[1]
    n_blocks = wd.shape[1]
    resident = dict(pipeline_mode=pl.Buffered(1))
    row = lambda cols: pl.BlockSpec((TM_FFN, cols), lambda i: (i, 0))
    in_specs = [row(d), row(wa), row(ob.shape[1]),
                pl.BlockSpec((None, d, d), lambda i: (layer, 0, 0), **resident),
                pl.BlockSpec((None, 1, d), lambda i: (layer, 0, 0)),
                pl.BlockSpec((None, 2, n_blocks, d, TF), lambda i: (layer, 0, 0, 0, 0), **resident),
                pl.BlockSpec((None, CONV_WIDTH, 2, n_blocks, TF), lambda i: (layer, 0, 0, 0, 0)),
                pl.BlockSpec((None, 2, n_blocks, TF), lambda i: (layer, 0, 0, 0)),
                pl.BlockSpec((None, n_blocks, TF, d), lambda i: (layer, 0, 0, 0), **resident)]
    args = [x2d, oa, ob, wout, g, wup, cw, cb, wd]
    if final_g is not None:
        in_specs.append(pl.BlockSpec((1, d), lambda i: (0, 0)))
        args.append(final_g)
    return pl.pallas_call(
        functools.partial(_ffn_kernel, tiles_per_seq=seq // TM_FFN,
                          final_norm=final_g is not None),
        out_shape=jax.ShapeDtypeStruct((t, d), jnp.float32),
        grid=(t // TM_FFN,),
        in_specs=in_specs,
        out_specs=row(d),
        scratch_shapes=[pltpu.VMEM((TM_FFN, d), jnp.bfloat16),
                        pltpu.VMEM((TM_FFN, d), jnp.float32),
                        pltpu.VMEM((SUBLANES + TM_FFN, TF), jnp.float32),
                        pltpu.VMEM((SUBLANES + TM_FFN, TF), jnp.float32),
                        pltpu.VMEM((2, n_blocks, SUBLANES, TF), jnp.float32)],
        compiler_params=pltpu.CompilerParams(dimension_semantics=("arbitrary",),
                                             vmem_limit_bytes=VMEM_LIMIT),
        name="outproj_convffn",
    )(*args)


def kernel(x, attn_norm_g, w_in, a_rel_bias, t5_bias, lambda_q1, lambda_k1, lambda_q2,
           lambda_k2, subln_g, w_out, ffn_norm_g, w_up, conv_w, conv_b, w_down, final_norm_g):
    b, s, d = x.shape
    depth = w_in.shape[0]
    d_ff = w_down.shape[1]
    n_blocks = d_ff // TF
    assert s % QB == 0 and s % TM_FFN == 0 and s % TM_PROJ == 0 and d_ff % TF == 0
    bf16 = jnp.bfloat16

    qa, ka, va, qb, kb, vb = jnp.split(w_in.astype(bf16), 6, axis=-1)
    w_qk = jnp.concatenate([qa, ka, qb, kb], axis=-1)
    w_vt = jnp.concatenate([va, vb], axis=-1).transpose(0, 2, 1)
    w_out_b = w_out.astype(bf16)
    w_up_b = w_up.astype(bf16).reshape(depth, d, 2, n_blocks, TF).transpose(0, 2, 3, 1, 4)
    w_down_b = w_down.astype(bf16).reshape(depth, n_blocks, TF, d)
    conv_w_r = conv_w.reshape(depth, CONV_WIDTH, 2, n_blocks, TF)
    conv_b_r = conv_b.reshape(depth, 2, n_blocks, TF)
    attn_g = attn_norm_g.reshape(depth, 1, d)
    ffn_g = ffn_norm_g.reshape(depth, 1, d)
    sub_g = subln_g.reshape(depth, 1, PAIR)
    lam_params = jnp.stack([lambda_q1, lambda_k1, lambda_q2, lambda_k2], axis=1)

    band_tab = _pad_entries(a_rel_bias.reshape(depth * A_HEADS, -1), 2 * A_MAX_REL + 1)
    band_bias = _expand(band_tab, jnp.asarray(_band_bias_index()), "expand_band_bias")
    band_bias = (band_bias.reshape(depth, A_HEADS // 2, 2, WA, QA).transpose(0, 1, 3, 2, 4)
                 .reshape(depth, A_HEADS // 2, WA, 2 * QA))
    t5_tab = _pad_entries(t5_bias.T, T5_BUCKETS)
    t5_exp = _expand(t5_tab, _t5_bias_index(), "expand_t5_bias")
    side_by_side = lambda t: (t.reshape(B_HEADS, 2, QB, QB).transpose(0, 2, 1, 3)
                              .reshape(B_HEADS, QB, 2 * QB))
    bias_d = side_by_side(t5_exp[:, :QB * QB])
    bias_p = side_by_side(t5_exp[:, QB * QB:2 * QB * QB])
    bias_far = jnp.broadcast_to(t5_exp[:, 2 * QB * QB].reshape(B_HEADS, 1, 2, 1),
                                (B_HEADS, 1, 2, QB)).reshape(B_HEADS, 1, 2 * QB)

    x2d = x.reshape(b * s, d)
    for layer in range(depth):
        lam_init = 0.8 - 0.6 * math.exp(-0.3 * layer)
        qk, vt = _inproj(x2d, attn_g, w_qk, w_vt, layer, b, s)
        qk = qk.reshape(b, s, -1)
        oa = _band_attention(qk, vt, band_bias, layer).reshape(b * s, -1)
        ob = _diff_attention(qk, vt, bias_d, bias_p, bias_far, lam_params, sub_g, layer,
                             lam_init).reshape(b * s, -1)
        final_g = final_norm_g.reshape(1, d) if layer == depth - 1 else None
        x2d = _ffn(x2d, oa, ob, w_out_b, ffn_g, w_up_b, conv_w_r, conv_b_r, w_down_b,
                   final_g, layer, s)
    return x2d.reshape(b, s, d)
```

```python
import functools
import math

import numpy as np
import jax
import jax.numpy as jnp
from jax import lax
from jax.experimental import pallas as pl
from jax.experimental.pallas import tpu as pltpu

CHUNK = 64
A_LEFT_CHUNKS = 8
HEAD_DIM = 64
A_HEADS = 8
B_HEADS = 4
A_MAX_REL = 128
T5_BUCKETS = 32
T5_MAX_DIST = 128
CONV_WIDTH = 3
RMS_EPS = 1e-6

LANES = 128
SUBLANES = 8
PAIR = 2 * HEAD_DIM
MASK_VALUE = -1e30
LOG2E = math.log2(math.e)
SCORE_SCALE = HEAD_DIM ** -0.5 * LOG2E

A_WIDTH = A_HEADS * HEAD_DIM
B_WIDTH = B_HEADS * PAIR
QA = 128
WA = A_LEFT_CHUNKS * CHUNK + QA
PAD_A = A_LEFT_CHUNKS * CHUNK
QB = 256
TM_PROJ = 512
TM_FFN = 512
TF = 256
UP_BUFFERS = 2
EXPAND_COLS = 2048

VMEM_LIMIT = 56 * 1024 * 1024


def _dot(a, b):
    return jnp.dot(a, b, preferred_element_type=jnp.float32)


def _dot_nt(a, b):
    return lax.dot_general(a, b, (((1,), (1,)), ((), ())),
                           preferred_element_type=jnp.float32)


def _rmsnorm(x, g):
    return x * lax.rsqrt(jnp.mean(x * x, axis=-1, keepdims=True) + RMS_EPS) * g


def _expand_kernel(tab_ref, idx_ref, o_ref, *, rows):
    n_entries = tab_ref.shape[1]
    cols = idx_ref.shape[1]
    entry = lax.broadcasted_iota(jnp.int32, (n_entries, cols), 0)
    onehot = jnp.where(entry == idx_ref[...], 1.0, 0.0).astype(jnp.bfloat16)
    parts = _dot(tab_ref[...], onehot)
    o_ref[...] = (parts[0:rows] + parts[rows:2 * rows]) + parts[2 * rows:3 * rows]


def _expand(table, idx, name):
    rows, n_entries = table.shape
    ncols = idx.shape[1]
    hi = table.astype(jnp.bfloat16)
    r1 = table - hi.astype(jnp.float32)
    mid = r1.astype(jnp.bfloat16)
    lo = (r1 - mid.astype(jnp.float32)).astype(jnp.bfloat16)
    tab3 = jnp.concatenate([hi, mid, lo], axis=0)
    return pl.pallas_call(
        functools.partial(_expand_kernel, rows=rows),
        out_shape=jax.ShapeDtypeStruct((rows, ncols), jnp.float32),
        grid=(ncols // EXPAND_COLS,),
        in_specs=[pl.BlockSpec((3 * rows, n_entries), lambda c: (0, 0)),
                  pl.BlockSpec((1, EXPAND_COLS), lambda c: (0, c))],
        out_specs=pl.BlockSpec((rows, EXPAND_COLS), lambda c: (0, c)),
        compiler_params=pltpu.CompilerParams(dimension_semantics=("arbitrary",)),
        name=name,
    )(tab3, idx)


def _pad_entries(table, mask_col):
    rows, n = table.shape
    total = -(-(n + 1) // LANES) * LANES
    out = jnp.zeros((rows, total), jnp.float32)
    out = out.at[:, :n].set(table)
    return out.at[:, mask_col].set(MASK_VALUE)


def _band_bias_index():
    m = np.arange(WA)[:, None]
    q = np.arange(QA)[None, :]
    dist = PAD_A + q - m
    idx = np.clip(dist, -A_MAX_REL, A_MAX_REL) + A_MAX_REL
    qc, kc = q // CHUNK, m // CHUNK
    visible = (kc >= qc) & (kc <= qc + A_LEFT_CHUNKS)
    mask_entry = 2 * A_MAX_REL + 1
    return np.where(visible, idx, mask_entry).astype(np.int32).reshape(1, WA * QA)


def _t5_bucket(rel):
    nb = T5_BUCKETS // 2
    ret = jnp.where(rel > 0, nb, 0)
    n = jnp.abs(rel)
    max_exact = nb // 2
    is_small = n < max_exact
    nf = jnp.maximum(n, 1).astype(jnp.float32)
    large = max_exact + (jnp.log(nf / max_exact) / math.log(T5_MAX_DIST / max_exact)
                         * (nb - max_exact)).astype(jnp.int32)
    large = jnp.minimum(large, nb - 1)
    return ret + jnp.where(is_small, n, large)


def _t5_bias_index():
    k = np.arange(QB)[:, None]
    q = np.arange(QB)[None, :]
    rel_d = (k - q).astype(np.int32)
    rel_p = (k - QB - q).astype(np.int32)
    diag = jnp.where(jnp.asarray(k // CHUNK <= q // CHUNK), _t5_bucket(jnp.asarray(rel_d)),
                     T5_BUCKETS)
    prev = _t5_bucket(jnp.asarray(rel_p))
    far = jnp.broadcast_to(_t5_bucket(jnp.asarray([-(QB + 1)], dtype=jnp.int32)),
                           (EXPAND_COLS,))
    return jnp.concatenate([diag.reshape(-1), prev.reshape(-1), far]).astype(jnp.int32)[None]


def _inproj_kernel(x_ref, g_ref, wqk_ref, wvt_ref, qk_ref, vt_ref, *, chunk):
    h = _rmsnorm(x_ref[...], g_ref[...]).astype(jnp.bfloat16)
    for c in range(wqk_ref.shape[1] // chunk):
        sl = slice(c * chunk, (c + 1) * chunk)
        out = _dot(h, wqk_ref[:, sl])
        if c % 2 == 0:
            out = out * SCORE_SCALE
        qk_ref[:, sl] = out.astype(qk_ref.dtype)
    for c in range(wvt_ref.shape[0] // chunk):
        sl = slice(c * chunk, (c + 1) * chunk)
        vt_ref[0, sl, :] = _dot_nt(wvt_ref[sl, :], h).astype(vt_ref.dtype)


def _inproj(x2d, g, wqk, wvt, layer, batch, seq):
    t, d = x2d.shape
    nqk, nv = wqk.shape[2], wvt.shape[1]
    tiles = seq // TM_PROJ
    return pl.pallas_call(
        functools.partial(_inproj_kernel, chunk=A_WIDTH),
        out_shape=(jax.ShapeDtypeStruct((t, nqk), jnp.bfloat16),
                   jax.ShapeDtypeStruct((batch, nv, seq), jnp.bfloat16)),
        grid=(t // TM_PROJ,),
        in_specs=[pl.BlockSpec((TM_PROJ, d), lambda i: (i, 0)),
                  pl.BlockSpec((None, 1, d), lambda i: (layer, 0, 0)),
                  pl.BlockSpec((None, d, nqk), lambda i: (layer, 0, 0)),
                  pl.BlockSpec((None, nv, d), lambda i: (layer, 0, 0))],
        out_specs=(pl.BlockSpec((TM_PROJ, nqk), lambda i: (i, 0)),
                   pl.BlockSpec((1, nv, TM_PROJ), lambda i: (i // tiles, 0, i % tiles))),
        compiler_params=pltpu.CompilerParams(dimension_semantics=("arbitrary",),
                                             vmem_limit_bytes=VMEM_LIMIT),
        name="inproj",
    )(x2d, g, wqk, wvt)


def _split_heads(q2):
    lane = lax.broadcasted_iota(jnp.int32, q2.shape, 1)
    zero = jnp.zeros_like(q2)
    return jnp.concatenate([jnp.where(lane < HEAD_DIM, q2, zero),
                            jnp.where(lane >= HEAD_DIM, q2, zero)], axis=0)


def _band_kernel(q_ref, k_ref, vt_ref, bias_ref, o_ref):
    n_blocks = q_ref.shape[1] // QA
    head_row = lax.broadcasted_iota(jnp.int32, (PAIR, QA), 0) < HEAD_DIM
    window = lambda i: (max(0, i * QA - PAD_A), (i + 1) * QA)

    def scores(i):
        lo, hi = window(i)
        qs = _split_heads(q_ref[0, i * QA:hi, :])
        return _dot_nt(k_ref[0, lo:hi, :], qs) + bias_ref[WA - (hi - lo):WA, :]

    def finish(i, s):
        lo, hi = window(i)
        m = jnp.max(s, axis=0, keepdims=True)
        e = jnp.exp2(s - m)
        l = jnp.sum(e, axis=0, keepdims=True)
        acc = _dot(vt_ref[0, :, lo:hi], e.astype(jnp.bfloat16))
        ot = acc * (1.0 / l)
        o = jnp.where(head_row, ot[:, 0:QA], ot[:, QA:2 * QA]).T
        o_ref[0, i * QA:hi, :] = o.astype(o_ref.dtype)

    s = scores(0)
    for i in range(n_blocks):
        s_next = scores(i + 1) if i + 1 < n_blocks else None
        finish(i, s)
        s = s_next


def _band_attention(qk, vt, bias, layer):
    b, s, _ = qk.shape
    pairs = A_HEADS // 2
    return pl.pallas_call(
        _band_kernel,
        out_shape=jax.ShapeDtypeStruct((b, s, A_WIDTH), jnp.bfloat16),
        grid=(b, pairs),
        in_specs=[pl.BlockSpec((1, s, PAIR), lambda bi, h: (bi, 0, h)),
                  pl.BlockSpec((1, s, PAIR), lambda bi, h: (bi, 0, pairs + h)),
                  pl.BlockSpec((1, PAIR, s), lambda bi, h: (bi, h, 0)),
                  pl.BlockSpec((None, None, WA, 2 * QA), lambda bi, h: (layer, h, 0, 0))],
        out_specs=pl.BlockSpec((1, s, PAIR), lambda bi, h: (bi, 0, h)),
        compiler_params=pltpu.CompilerParams(dimension_semantics=("arbitrary", "arbitrary"),
                                             vmem_limit_bytes=VMEM_LIMIT),
        name="band_attn",
    )(qk, qk, vt, bias)


def _diff_kernel(q_ref, k_ref, vt_ref, bd_ref, bp_ref, cf_ref, lam_ref, g_ref, o_ref,
                 *, lam_init):
    n_blocks = q_ref.shape[1] // QB
    lp = lam_ref[...]
    lam = (jnp.exp(jnp.sum(lp[0:1] * lp[1:2], axis=1, keepdims=True))
           - jnp.exp(jnp.sum(lp[2:3] * lp[3:4], axis=1, keepdims=True)) + lam_init)
    far_bias = cf_ref[...]

    def scores(i):
        qs = _split_heads(q_ref[0, i * QB:(i + 1) * QB, :])
        return _dot_nt(k_ref[0, 0:(i + 1) * QB, :], qs)

    def finish(i, s):
        klen = (i + 1) * QB
        near0 = max(0, klen - 2 * QB)
        if i == 0:
            s_near = s + bd_ref[...]
        else:
            s_near = s[near0:klen] + jnp.concatenate([bp_ref[...], bd_ref[...]], axis=0)
        m = jnp.max(s_near, axis=0, keepdims=True)
        if near0 > 0:
            s_far = s[0:near0]
            m = jnp.maximum(m, jnp.max(s_far, axis=0, keepdims=True) + far_bias)
        e_near = jnp.exp2(s_near - m)
        l = jnp.sum(e_near, axis=0, keepdims=True)
        acc = _dot(vt_ref[0, :, near0:klen], e_near.astype(jnp.bfloat16))
        if near0 > 0:
            e_far = jnp.exp2(s_far - (m - far_bias))
            l = l + jnp.sum(e_far, axis=0, keepdims=True)
            acc = acc + _dot(vt_ref[0, :, 0:near0], e_far.astype(jnp.bfloat16))
        ot = acc * (1.0 / l)
        o = (ot[:, 0:QB] - lam * ot[:, QB:2 * QB]).T
        o = _rmsnorm(o, g_ref[...]) * (1.0 - lam_init)
        o_ref[0, i * QB:klen, :] = o.astype(o_ref.dtype)

    s = scores(0)
    for i in range(n_blocks):
        s_next = scores(i + 1) if i + 1 < n_blocks else None
        finish(i, s)
        s = s_next


def _diff_attention(qk, vt, bias_d, bias_p, bias_far, lam_params, subln_g, layer, lam_init):
    b, s, _ = qk.shape
    first = 2 * A_WIDTH // PAIR
    tile = pl.BlockSpec((None, QB, 2 * QB), lambda bi, h: (h, 0, 0))
    return pl.pallas_call(
        functools.partial(_diff_kernel, lam_init=lam_init),
        out_shape=jax.ShapeDtypeStruct((b, s, B_WIDTH), jnp.bfloat16),
        grid=(b, B_HEADS),
        in_specs=[pl.BlockSpec((1, s, PAIR), lambda bi, h: (bi, 0, first + h)),
                  pl.BlockSpec((1, s, PAIR), lambda bi, h: (bi, 0, first + B_HEADS + h)),
                  pl.BlockSpec((1, PAIR, s), lambda bi, h: (bi, A_WIDTH // PAIR + h, 0)),
                  tile, tile,
                  pl.BlockSpec((None, 1, 2 * QB), lambda bi, h: (h, 0, 0)),
                  pl.BlockSpec((None, 4, HEAD_DIM), lambda bi, h: (layer, 0, 0)),
                  pl.BlockSpec((None, 1, PAIR), lambda bi, h: (layer, 0, 0))],
        out_specs=pl.BlockSpec((1, s, PAIR), lambda bi, h: (bi, 0, h)),
        compiler_params=pltpu.CompilerParams(dimension_semantics=("arbitrary", "arbitrary"),
                                             vmem_limit_bytes=VMEM_LIMIT),
        name="diff_attn",
    )(qk, qk, vt, bias_d, bias_p, bias_far, lam_params, subln_g)


def _ffn_kernel(*refs, tiles_per_seq, final_norm):
    if final_norm:
        (x_ref, oa_ref, ob_ref, wout_ref, g_ref, wup_ref, cw_ref, cb_ref, wd_ref, fg_ref,
         out_ref, h_s, x1_s, act_s, up_s, carry_s) = refs
    else:
        (x_ref, oa_ref, ob_ref, wout_ref, g_ref, wup_ref, cw_ref, cb_ref, wd_ref,
         out_ref, h_s, x1_s, act_s, up_s, carry_s) = refs
    tm = x_ref.shape[0]
    n_blocks = wup_ref.shape[1]
    wa = oa_ref.shape[1]
    halo = SUBLANES

    @pl.when(pl.program_id(0) % tiles_per_seq == 0)
    def _():
        carry_s[...] = jnp.zeros_like(carry_s)

    x1 = x_ref[...] + _dot(oa_ref[...], wout_ref[0:wa, :]) + _dot(ob_ref[...], wout_ref[wa:, :])
    h_s[...] = _rmsnorm(x1, g_ref[...]).astype(h_s.dtype)
    x1_s[...] = x1

    def conv(up, part, j):
        buf = up_s.at[j % UP_BUFFERS, part]
        buf[0:halo, :] = carry_s[part, j]
        buf[halo:halo + tm, :] = up
        carry_s[part, j] = up[tm - halo:tm, :]
        tap = lambda t: cw_ref[t, part, j:j + 1, :]
        out = cb_ref[part, j:j + 1, :] + buf[halo - 2:halo - 2 + tm, :] * tap(0)
        out = out + buf[halo - 1:halo - 1 + tm, :] * tap(1)
        return out + up * tap(2)

    for j in range(n_blocks):
        h = h_s[...]
        gate = conv(_dot(h, wup_ref[0, j]), 0, j)
        val = conv(_dot(h, wup_ref[1, j]), 1, j)
        act = gate * (1.0 / (1.0 + jnp.exp(-gate))) * val
        act_s[:, j * TF:(j + 1) * TF] = act.astype(act_s.dtype)

    out = x1_s[...] + _dot(act_s[...], wd_ref[...])
    if final_norm:
        out = _rmsnorm(out, fg_ref[...])
    out_ref[...] = out


def _ffn(x2d, oa, ob, wout, g, wup, cw, cb, wd, final_g, layer, seq):
    t, d = x2d.shape
    wa = oa.shape[1]
    n_blocks = wup.shape[2]
    d_ff = wd.shape[1]
    resident = dict(pipeline_mode=pl.Buffered(1))
    row = lambda cols: pl.BlockSpec((TM_FFN, cols), lambda i: (i, 0))
    in_specs = [row(d), row(wa), row(ob.shape[1]),
                pl.BlockSpec((None, d, d), lambda i: (layer, 0, 0), **resident),
                pl.BlockSpec((None, 1, d), lambda i: (layer, 0, 0)),
                pl.BlockSpec((None, 2, n_blocks, d, TF), lambda i: (layer, 0, 0, 0, 0), **resident),
                pl.BlockSpec((None, CONV_WIDTH, 2, n_blocks, TF), lambda i: (layer, 0, 0, 0, 0)),
                pl.BlockSpec((None, 2, n_blocks, TF), lambda i: (layer, 0, 0, 0)),
                pl.BlockSpec((None, d_ff, d), lambda i: (layer, 0, 0), **resident)]
    args = [x2d, oa, ob, wout, g, wup, cw, cb, wd]
    if final_g is not None:
        in_specs.append(pl.BlockSpec((1, d), lambda i: (0, 0)))
        args.append(final_g)
    return pl.pallas_call(
        functools.partial(_ffn_kernel, tiles_per_seq=seq // TM_FFN,
                          final_norm=final_g is not None),
        out_shape=jax.ShapeDtypeStruct((t, d), jnp.float32),
        grid=(t // TM_FFN,),
        in_specs=in_specs,
        out_specs=row(d),
        scratch_shapes=[pltpu.VMEM((TM_FFN, d), jnp.bfloat16),
                        pltpu.VMEM((TM_FFN, d), jnp.float32),
                        pltpu.VMEM((TM_FFN, d_ff), jnp.bfloat16),
                        pltpu.VMEM((UP_BUFFERS, 2, SUBLANES + TM_FFN, TF), jnp.float32),
                        pltpu.VMEM((2, n_blocks, SUBLANES, TF), jnp.float32)],
        compiler_params=pltpu.CompilerParams(dimension_semantics=("arbitrary",),
                                             vmem_limit_bytes=VMEM_LIMIT),
        name="outproj_convffn",
    )(*args)


def kernel(x, attn_norm_g, w_in, a_rel_bias, t5_bias, lambda_q1, lambda_k1, lambda_q2,
           lambda_k2, subln_g, w_out, ffn_norm_g, w_up, conv_w, conv_b, w_down, final_norm_g):
    b, s, d = x.shape
    depth = w_in.shape[0]
    d_ff = w_down.shape[1]
    n_blocks = d_ff // TF
    assert s % QB == 0 and s % TM_FFN == 0 and s % TM_PROJ == 0 and d_ff % TF == 0
    assert A_WIDTH == B_WIDTH
    bf16 = jnp.bfloat16

    qa, ka, va, qb, kb, vb = jnp.split(w_in.astype(bf16), 6, axis=-1)
    w_qk = jnp.concatenate([qa, ka, qb, kb], axis=-1)
    w_vt = jnp.concatenate([va, vb], axis=-1).transpose(0, 2, 1)
    w_out_b = w_out.astype(bf16)
    w_up_b = w_up.astype(bf16).reshape(depth, d, 2, n_blocks, TF).transpose(0, 2, 3, 1, 4)
    w_down_b = w_down.astype(bf16)
    conv_w_r = conv_w.reshape(depth, CONV_WIDTH, 2, n_blocks, TF)
    conv_b_r = conv_b.reshape(depth, 2, n_blocks, TF)
    attn_g = attn_norm_g.reshape(depth, 1, d)
    ffn_g = ffn_norm_g.reshape(depth, 1, d)
    sub_g = subln_g.reshape(depth, 1, PAIR)
    lam_params = jnp.stack([lambda_q1, lambda_k1, lambda_q2, lambda_k2], axis=1)

    band_tab = _pad_entries(a_rel_bias.reshape(depth * A_HEADS, -1) * LOG2E, 2 * A_MAX_REL + 1)
    band_bias = _expand(band_tab, jnp.asarray(_band_bias_index()), "expand_band_bias")
    band_bias = (band_bias.reshape(depth, A_HEADS // 2, 2, WA, QA).transpose(0, 1, 3, 2, 4)
                 .reshape(depth, A_HEADS // 2, WA, 2 * QA))
    t5_tab = _pad_entries(t5_bias.T * LOG2E, T5_BUCKETS)
    t5_exp = _expand(t5_tab, _t5_bias_index(), "expand_t5_bias")
    side_by_side = lambda t: (t.reshape(B_HEADS, 2, QB, QB).transpose(0, 2, 1, 3)
                              .reshape(B_HEADS, QB, 2 * QB))
    bias_d = side_by_side(t5_exp[:, :QB * QB])
    bias_p = side_by_side(t5_exp[:, QB * QB:2 * QB * QB])
    bias_far = jnp.broadcast_to(t5_exp[:, 2 * QB * QB].reshape(B_HEADS, 1, 2, 1),
                                (B_HEADS, 1, 2, QB)).reshape(B_HEADS, 1, 2 * QB)

    x2d = x.reshape(b * s, d)
    for layer in range(depth):
        lam_init = 0.8 - 0.6 * math.exp(-0.3 * layer)
        qk, vt = _inproj(x2d, attn_g, w_qk, w_vt, layer, b, s)
        qk = qk.reshape(b, s, -1)
        oa = _band_attention(qk, vt, band_bias, layer).reshape(b * s, -1)
        ob = _diff_attention(qk, vt, bias_d, bias_p, bias_far, lam_params, sub_g, layer,
                             lam_init).reshape(b * s, -1)
        final_g = final_norm_g.reshape(1, d) if layer == depth - 1 else None
        x2d = _ffn(x2d, oa, ob, w_out_b, ffn_g, w_up_b, conv_w_r, conv_b_r, w_down_b,
                   final_g, layer, s)
    return x2d.reshape(b, s, d)
```

```python
import functools
import math

import numpy as np
import jax
import jax.numpy as jnp
from jax import lax
from jax.experimental import pallas as pl
from jax.experimental.pallas import tpu as pltpu

CHUNK = 64
A_LEFT_CHUNKS = 8
HEAD_DIM = 64
A_HEADS = 8
B_HEADS = 4
A_MAX_REL = 128
T5_BUCKETS = 32
T5_MAX_DIST = 128
CONV_WIDTH = 3
RMS_EPS = 1e-6

LANES = 128
SUBLANES = 8
PAIR = 2 * HEAD_DIM
MASK_VALUE = -1e30
LOG2E = math.log2(math.e)
SCORE_SCALE = HEAD_DIM ** -0.5 * LOG2E

A_WIDTH = A_HEADS * HEAD_DIM
B_WIDTH = B_HEADS * PAIR
QA = 128
WA = A_LEFT_CHUNKS * CHUNK + QA
PAD_A = A_LEFT_CHUNKS * CHUNK
QB = 256
TM_PROJ = 512
TM_FFN = 512
TF = 256
EXPAND_COLS = 4096

VMEM_LIMIT = 56 * 1024 * 1024


def _dot(a, b):
    return jnp.dot(a, b, preferred_element_type=jnp.float32)


def _dot_nt(a, b):
    return lax.dot_general(a, b, (((1,), (1,)), ((), ())),
                           preferred_element_type=jnp.float32)


def _rmsnorm(x, g):
    return x * lax.rsqrt(jnp.mean(x * x, axis=-1, keepdims=True) + RMS_EPS) * g


def _expand_kernel(tab_ref, idx_ref, o_ref, *, rows):
    n_entries = tab_ref.shape[1]
    cols = idx_ref.shape[1]
    entry = lax.broadcasted_iota(jnp.int32, (n_entries, cols), 0)
    onehot = jnp.where(entry == idx_ref[...], 1.0, 0.0).astype(jnp.bfloat16)
    parts = _dot(tab_ref[...], onehot)
    o_ref[...] = (parts[0:rows] + parts[rows:2 * rows]) + parts[2 * rows:3 * rows]


def _expand(table, idx, name):
    rows, n_entries = table.shape
    ncols = idx.shape[1]
    hi = table.astype(jnp.bfloat16)
    r1 = table - hi.astype(jnp.float32)
    mid = r1.astype(jnp.bfloat16)
    lo = (r1 - mid.astype(jnp.float32)).astype(jnp.bfloat16)
    tab3 = jnp.concatenate([hi, mid, lo], axis=0)
    return pl.pallas_call(
        functools.partial(_expand_kernel, rows=rows),
        out_shape=jax.ShapeDtypeStruct((rows, ncols), jnp.float32),
        grid=(ncols // EXPAND_COLS,),
        in_specs=[pl.BlockSpec((3 * rows, n_entries), lambda c: (0, 0)),
                  pl.BlockSpec((1, EXPAND_COLS), lambda c: (0, c))],
        out_specs=pl.BlockSpec((rows, EXPAND_COLS), lambda c: (0, c)),
        compiler_params=pltpu.CompilerParams(dimension_semantics=("arbitrary",)),
        name=name,
    )(tab3, idx)


def _pad_entries(table, mask_col):
    rows, n = table.shape
    total = -(-(n + 1) // LANES) * LANES
    out = jnp.zeros((rows, total), jnp.float32)
    out = out.at[:, :n].set(table)
    return out.at[:, mask_col].set(MASK_VALUE)


def _band_bias_index():
    m = np.arange(WA)[:, None]
    q = np.arange(QA)[None, :]
    dist = PAD_A + q - m
    idx = np.clip(dist, -A_MAX_REL, A_MAX_REL) + A_MAX_REL
    qc, kc = q // CHUNK, m // CHUNK
    visible = (kc >= qc) & (kc <= qc + A_LEFT_CHUNKS)
    mask_entry = 2 * A_MAX_REL + 1
    return np.where(visible, idx, mask_entry).astype(np.int32).reshape(1, WA * QA)


def _t5_bucket(rel):
    nb = T5_BUCKETS // 2
    ret = jnp.where(rel > 0, nb, 0)
    n = jnp.abs(rel)
    max_exact = nb // 2
    is_small = n < max_exact
    nf = jnp.maximum(n, 1).astype(jnp.float32)
    large = max_exact + (jnp.log(nf / max_exact) / math.log(T5_MAX_DIST / max_exact)
                         * (nb - max_exact)).astype(jnp.int32)
    large = jnp.minimum(large, nb - 1)
    return ret + jnp.where(is_small, n, large)


def _t5_bias_index():
    k = np.arange(QB)[:, None]
    q = np.arange(QB)[None, :]
    rel_d = (k - q).astype(np.int32)
    rel_p = (k - QB - q).astype(np.int32)
    diag = jnp.where(jnp.asarray(k // CHUNK <= q // CHUNK), _t5_bucket(jnp.asarray(rel_d)),
                     T5_BUCKETS)
    prev = _t5_bucket(jnp.asarray(rel_p))
    far = jnp.broadcast_to(_t5_bucket(jnp.asarray([-(QB + 1)], dtype=jnp.int32)),
                           (EXPAND_COLS,))
    return jnp.concatenate([diag.reshape(-1), prev.reshape(-1), far]).astype(jnp.int32)[None]


def _inproj_kernel(x_ref, g_ref, win_ref, wvt_ref, qk_ref, vt_ref, *, chunk):
    h = _rmsnorm(x_ref[...], g_ref[...]).astype(jnp.bfloat16)
    for c, src in enumerate((0, 1, 3, 4)):
        out = _dot(h, win_ref[:, src * chunk:(src + 1) * chunk])
        if c % 2 == 0:
            out = out * SCORE_SCALE
        qk_ref[:, c * chunk:(c + 1) * chunk] = out.astype(qk_ref.dtype)
    for c in range(wvt_ref.shape[0] // chunk):
        sl = slice(c * chunk, (c + 1) * chunk)
        vt_ref[0, sl, :] = _dot_nt(wvt_ref[sl, :], h).astype(vt_ref.dtype)


def _inproj(x2d, g, win, wvt, layer, batch, seq):
    t, d = x2d.shape
    nv = wvt.shape[1]
    nqk = win.shape[2] - nv
    tiles = seq // TM_PROJ
    return pl.pallas_call(
        functools.partial(_inproj_kernel, chunk=A_WIDTH),
        out_shape=(jax.ShapeDtypeStruct((t, nqk), jnp.bfloat16),
                   jax.ShapeDtypeStruct((batch, nv, seq), jnp.bfloat16)),
        grid=(t // TM_PROJ,),
        in_specs=[pl.BlockSpec((TM_PROJ, d), lambda i: (i, 0)),
                  pl.BlockSpec((None, 1, d), lambda i: (layer, 0, 0)),
                  pl.BlockSpec((None, d, win.shape[2]), lambda i: (layer, 0, 0)),
                  pl.BlockSpec((None, nv, d), lambda i: (layer, 0, 0))],
        out_specs=(pl.BlockSpec((TM_PROJ, nqk), lambda i: (i, 0)),
                   pl.BlockSpec((1, nv, TM_PROJ), lambda i: (i // tiles, 0, i % tiles))),
        compiler_params=pltpu.CompilerParams(dimension_semantics=("arbitrary",),
                                             vmem_limit_bytes=VMEM_LIMIT),
        name="inproj",
    )(x2d, g, win, wvt)


def _split_heads(q2):
    lane = lax.broadcasted_iota(jnp.int32, q2.shape, 1)
    zero = jnp.zeros_like(q2)
    return jnp.concatenate([jnp.where(lane < HEAD_DIM, q2, zero),
                            jnp.where(lane >= HEAD_DIM, q2, zero)], axis=0)


def _band_kernel(q_ref, k_ref, vt_ref, bias_ref, o_ref):
    n_blocks = q_ref.shape[1] // QA
    head_row = lax.broadcasted_iota(jnp.int32, (PAIR, QA), 0) < HEAD_DIM
    window = lambda i: (max(0, i * QA - PAD_A), (i + 1) * QA)

    def scores(i):
        lo, hi = window(i)
        qs = _split_heads(q_ref[0, i * QA:hi, :])
        return _dot_nt(k_ref[0, lo:hi, :], qs) + bias_ref[WA - (hi - lo):WA, :]

    def softmax_stage(s):
        m = jnp.max(s, axis=0, keepdims=True)
        e = jnp.exp2(s - m)
        return e.astype(jnp.bfloat16), jnp.sum(e, axis=0, keepdims=True)

    def value_stage(i, e, l):
        lo, hi = window(i)
        acc = _dot(vt_ref[0, :, lo:hi], e)
        ot = acc * (1.0 / l)
        o = jnp.where(head_row, ot[:, 0:QA], ot[:, QA:2 * QA]).T
        o_ref[0, i * QA:hi, :] = o.astype(o_ref.dtype)

    s = scores(0)
    probs = None
    for i in range(n_blocks):
        s_next = scores(i + 1) if i + 1 < n_blocks else None
        if i > 0:
            value_stage(i - 1, *probs)
        probs = softmax_stage(s)
        s = s_next
    value_stage(n_blocks - 1, *probs)


def _band_attention(qk, vt, bias, layer):
    b, s, _ = qk.shape
    pairs = A_HEADS // 2
    return pl.pallas_call(
        _band_kernel,
        out_shape=jax.ShapeDtypeStruct((b, s, A_WIDTH), jnp.bfloat16),
        grid=(b, pairs),
        in_specs=[pl.BlockSpec((1, s, PAIR), lambda bi, h: (bi, 0, h)),
                  pl.BlockSpec((1, s, PAIR), lambda bi, h: (bi, 0, pairs + h)),
                  pl.BlockSpec((1, PAIR, s), lambda bi, h: (bi, h, 0)),
                  pl.BlockSpec((None, None, WA, 2 * QA), lambda bi, h: (layer, h, 0, 0))],
        out_specs=pl.BlockSpec((1, s, PAIR), lambda bi, h: (bi, 0, h)),
        compiler_params=pltpu.CompilerParams(dimension_semantics=("arbitrary", "arbitrary"),
                                             vmem_limit_bytes=VMEM_LIMIT),
        name="band_attn",
    )(qk, qk, vt, bias)


def _diff_kernel(q_ref, k_ref, vt_ref, bd_ref, bp_ref, cf_ref, lam_ref, g_ref, o_ref,
                 *, lam_init):
    n_blocks = q_ref.shape[1] // QB
    lp = lam_ref[...]
    lam = (jnp.exp(jnp.sum(lp[0:1] * lp[1:2], axis=1, keepdims=True))
           - jnp.exp(jnp.sum(lp[2:3] * lp[3:4], axis=1, keepdims=True)) + lam_init)
    far_bias = cf_ref[...]

    def scores(i):
        qs = _split_heads(q_ref[0, i * QB:(i + 1) * QB, :])
        return _dot_nt(k_ref[0, 0:(i + 1) * QB, :], qs)

    def softmax_stage(i, s):
        klen = (i + 1) * QB
        near0 = max(0, klen - 2 * QB)
        if i == 0:
            s_near = s + bd_ref[...]
        else:
            s_near = s[near0:klen] + jnp.concatenate([bp_ref[...], bd_ref[...]], axis=0)
        m = jnp.max(s_near, axis=0, keepdims=True)
        if near0 > 0:
            s_far = s[0:near0]
            m = jnp.maximum(m, jnp.max(s_far, axis=0, keepdims=True) + far_bias)
        e_near = jnp.exp2(s_near - m)
        l = jnp.sum(e_near, axis=0, keepdims=True)
        e_far = None
        if near0 > 0:
            e_far = jnp.exp2(s_far - (m - far_bias))
            l = l + jnp.sum(e_far, axis=0, keepdims=True)
            e_far = e_far.astype(jnp.bfloat16)
        return e_near.astype(jnp.bfloat16), e_far, l

    def value_stage(i, e_near, e_far, l):
        klen = (i + 1) * QB
        near0 = max(0, klen - 2 * QB)
        acc = _dot(vt_ref[0, :, near0:klen], e_near)
        if e_far is not None:
            acc = acc + _dot(vt_ref[0, :, 0:near0], e_far)
        ot = acc * (1.0 / l)
        o = (ot[:, 0:QB] - lam * ot[:, QB:2 * QB]).T
        o = _rmsnorm(o, g_ref[...]) * (1.0 - lam_init)
        o_ref[0, i * QB:klen, :] = o.astype(o_ref.dtype)

    s = scores(0)
    probs = None
    for i in range(n_blocks):
        s_next = scores(i + 1) if i + 1 < n_blocks else None
        if i > 0:
            value_stage(i - 1, *probs)
        probs = softmax_stage(i, s)
        s = s_next
    value_stage(n_blocks - 1, *probs)


def _diff_attention(qk, vt, bias_d, bias_p, bias_far, lam_params, subln_g, layer, lam_init):
    b, s, _ = qk.shape
    first = 2 * A_WIDTH // PAIR
    tile = pl.BlockSpec((None, QB, 2 * QB), lambda bi, h: (h, 0, 0))
    return pl.pallas_call(
        functools.partial(_diff_kernel, lam_init=lam_init),
        out_shape=jax.ShapeDtypeStruct((b, s, B_WIDTH), jnp.bfloat16),
        grid=(b, B_HEADS),
        in_specs=[pl.BlockSpec((1, s, PAIR), lambda bi, h: (bi, 0, first + h)),
                  pl.BlockSpec((1, s, PAIR), lambda bi, h: (bi, 0, first + B_HEADS + h)),
                  pl.BlockSpec((1, PAIR, s), lambda bi, h: (bi, A_WIDTH // PAIR + h, 0)),
                  tile, tile,
                  pl.BlockSpec((None, 1, 2 * QB), lambda bi, h: (h, 0, 0)),
                  pl.BlockSpec((None, 4, HEAD_DIM), lambda bi, h: (layer, 0, 0)),
                  pl.BlockSpec((None, 1, PAIR), lambda bi, h: (layer, 0, 0))],
        out_specs=pl.BlockSpec((1, s, PAIR), lambda bi, h: (bi, 0, h)),
        compiler_params=pltpu.CompilerParams(dimension_semantics=("arbitrary", "arbitrary"),
                                             vmem_limit_bytes=VMEM_LIMIT),
        name="diff_attn",
    )(qk, qk, vt, bias_d, bias_p, bias_far, lam_params, subln_g)


def _ffn_kernel(*refs, tiles_per_seq, final_norm):
    if final_norm:
        (x_ref, oa_ref, ob_ref, wout_ref, g_ref, wup_ref, cw_ref, cb_ref, wd_ref, fg_ref,
         out_ref, h_s, x1_s, act_s, carry_s) = refs
    else:
        (x_ref, oa_ref, ob_ref, wout_ref, g_ref, wup_ref, cw_ref, cb_ref, wd_ref,
         out_ref, h_s, x1_s, act_s, carry_s) = refs
    tm = x_ref.shape[0]
    d_ff = wd_ref.shape[0]
    n_blocks = d_ff // TF
    wa = oa_ref.shape[1]
    halo = SUBLANES

    @pl.when(pl.program_id(0) % tiles_per_seq == 0)
    def _():
        carry_s[...] = jnp.zeros_like(carry_s)

    x1 = x_ref[...] + _dot(oa_ref[...], wout_ref[0:wa, :]) + _dot(ob_ref[...], wout_ref[wa:, :])
    h_s[...] = _rmsnorm(x1, g_ref[...]).astype(h_s.dtype)
    x1_s[...] = x1

    first_rows = lax.broadcasted_iota(jnp.int32, (halo, TF), 0)

    def conv(up, part, j):
        prev = carry_s[part, j]
        carry_s[part, j] = up[tm - halo:tm, :]
        cols = slice(part * d_ff + j * TF, part * d_ff + (j + 1) * TF)
        tap = lambda t: cw_ref[t:t + 1, cols]

        def shifted(k):
            r = pltpu.roll(up, k, axis=0)
            head = jnp.where(first_rows < k, pltpu.roll(prev, k, axis=0), r[0:halo])
            return jnp.concatenate([head, r[halo:]], axis=0)

        out = cb_ref[:, cols] + shifted(2) * tap(0)
        out = out + shifted(1) * tap(1)
        return out + up * tap(2)

    for j in range(n_blocks):
        h = h_s[...]
        gate = conv(_dot(h, wup_ref[:, j * TF:(j + 1) * TF]), 0, j)
        val = conv(_dot(h, wup_ref[:, d_ff + j * TF:d_ff + (j + 1) * TF]), 1, j)
        act = gate * (1.0 / (1.0 + jnp.exp(-gate))) * val
        act_s[:, j * TF:(j + 1) * TF] = act.astype(act_s.dtype)

    out = x1_s[...] + _dot(act_s[...], wd_ref[...])
    if final_norm:
        out = _rmsnorm(out, fg_ref[...])
    out_ref[...] = out


def _ffn(x2d, oa, ob, wout, g, wup, cw, cb, wd, final_g, layer, seq):
    t, d = x2d.shape
    wa = oa.shape[1]
    d_ff = wd.shape[1]
    n_blocks = d_ff // TF
    resident = dict(pipeline_mode=pl.Buffered(1))
    row = lambda cols: pl.BlockSpec((TM_FFN, cols), lambda i: (i, 0))
    in_specs = [row(d), row(wa), row(ob.shape[1]),
                pl.BlockSpec((None, d, d), lambda i: (layer, 0, 0), **resident),
                pl.BlockSpec((None, 1, d), lambda i: (layer, 0, 0)),
                pl.BlockSpec((None, d, 2 * d_ff), lambda i: (layer, 0, 0), **resident),
                pl.BlockSpec((None, CONV_WIDTH, 2 * d_ff), lambda i: (layer, 0, 0)),
                pl.BlockSpec((None, 1, 2 * d_ff), lambda i: (layer, 0, 0)),
                pl.BlockSpec((None, d_ff, d), lambda i: (layer, 0, 0), **resident)]
    args = [x2d, oa, ob, wout, g, wup, cw, cb, wd]
    if final_g is not None:
        in_specs.append(pl.BlockSpec((1, d), lambda i: (0, 0)))
        args.append(final_g)
    return pl.pallas_call(
        functools.partial(_ffn_kernel, tiles_per_seq=seq // TM_FFN,
                          final_norm=final_g is not None),
        out_shape=jax.ShapeDtypeStruct((t, d), jnp.float32),
        grid=(t // TM_FFN,),
        in_specs=in_specs,
        out_specs=row(d),
        scratch_shapes=[pltpu.VMEM((TM_FFN, d), jnp.bfloat16),
                        pltpu.VMEM((TM_FFN, d), jnp.float32),
                        pltpu.VMEM((TM_FFN, d_ff), jnp.bfloat16),
                        pltpu.VMEM((2, n_blocks, SUBLANES, TF), jnp.float32)],
        compiler_params=pltpu.CompilerParams(dimension_semantics=("arbitrary",),
                                             vmem_limit_bytes=VMEM_LIMIT),
        name="outproj_convffn",
    )(*args)


def kernel(x, attn_norm_g, w_in, a_rel_bias, t5_bias, lambda_q1, lambda_k1, lambda_q2,
           lambda_k2, subln_g, w_out, ffn_norm_g, w_up, conv_w, conv_b, w_down, final_norm_g):
    b, s, d = x.shape
    depth = w_in.shape[0]
    d_ff = w_down.shape[1]
    n_blocks = d_ff // TF
    assert s % QB == 0 and s % TM_FFN == 0 and s % TM_PROJ == 0 and d_ff % TF == 0
    assert A_WIDTH == B_WIDTH
    bf16 = jnp.bfloat16

    w_in_b = w_in.astype(bf16)
    w_vt = jnp.concatenate([w_in_b[..., 2 * A_WIDTH:3 * A_WIDTH], w_in_b[..., -B_WIDTH:]],
                           axis=-1).transpose(0, 2, 1)
    w_out_b = w_out.astype(bf16)
    w_up_b = w_up.astype(bf16)
    w_down_b = w_down.astype(bf16)
    conv_b_r = conv_b.reshape(depth, 1, 2 * d_ff)
    attn_g = attn_norm_g.reshape(depth, 1, d)
    ffn_g = ffn_norm_g.reshape(depth, 1, d)
    sub_g = subln_g.reshape(depth, 1, PAIR)
    lam_params = jnp.stack([lambda_q1, lambda_k1, lambda_q2, lambda_k2], axis=1)

    band_tab = _pad_entries(a_rel_bias.reshape(depth * A_HEADS, -1) * LOG2E, 2 * A_MAX_REL + 1)
    band_bias = _expand(band_tab, jnp.asarray(_band_bias_index()), "expand_band_bias")
    band_bias = (band_bias.reshape(depth, A_HEADS // 2, 2, WA, QA).transpose(0, 1, 3, 2, 4)
                 .reshape(depth, A_HEADS // 2, WA, 2 * QA))
    t5_tab = _pad_entries(t5_bias.T * LOG2E, T5_BUCKETS)
    t5_exp = _expand(t5_tab, _t5_bias_index(), "expand_t5_bias")
    side_by_side = lambda t: (t.reshape(B_HEADS, 2, QB, QB).transpose(0, 2, 1, 3)
                              .reshape(B_HEADS, QB, 2 * QB))
    bias_d = side_by_side(t5_exp[:, :QB * QB])
    bias_p = side_by_side(t5_exp[:, QB * QB:2 * QB * QB])
    bias_far = jnp.broadcast_to(t5_exp[:, 2 * QB * QB].reshape(B_HEADS, 1, 2, 1),
                                (B_HEADS, 1, 2, QB)).reshape(B_HEADS, 1, 2 * QB)

    x2d = x.reshape(b * s, d)
    for layer in range(depth):
        lam_init = 0.8 - 0.6 * math.exp(-0.3 * layer)
        qk, vt = _inproj(x2d, attn_g, w_in_b, w_vt, layer, b, s)
        qk = qk.reshape(b, s, -1)
        oa = _band_attention(qk, vt, band_bias, layer).reshape(b * s, -1)
        ob = _diff_attention(qk, vt, bias_d, bias_p, bias_far, lam_params, sub_g, layer,
                             lam_init).reshape(b * s, -1)
        final_g = final_norm_g.reshape(1, d) if layer == depth - 1 else None
        x2d = _ffn(x2d, oa, ob, w_out_b, ffn_g, w_up_b, conv_w, conv_b_r, w_down_b,
                   final_g, layer, s)
    return x2d.reshape(b, s, d)
```
